```python
import jax, jax.numpy as jnp
from jax import lax
import numpy as np

D_MODEL = 2048
BATCH = 2
SEQ = 8192
DEPTH = 4
DEC_BATCH = 1
DEC_SEQ = 16384
PAST_LEN = 128

N_MIXERS = 2
N_POOL_LAYERS = (DEPTH + 1) // 2
N_SGU_LAYERS = DEPTH // 2
POOL_WINDOWS = (2, 4, 8, 16)
N_POOL_GROUPS = len(POOL_WINDOWS)
POOL_GROUP = D_MODEL // N_POOL_GROUPS
SGU_CHUNK = 128
SGU_FF = 6 * D_MODEL
SGU_HALF = SGU_FF // 2
N_SGU_HEADS = 8
SGU_HEAD_DIM = SGU_HALF // N_SGU_HEADS
D_FF = 4 * D_MODEL
N_MOD = 6
EPS = 1e-6

kernel_name = "hybrid_pool_sgu_adaln_encoder"


def _rmsnorm(x, g):
    xf = x.astype(jnp.float32)
    y = xf * lax.rsqrt(jnp.mean(xf * xf, axis=-1, keepdims=True) + EPS)
    return (y * g.astype(jnp.float32)).astype(x.dtype)


def _modulate(h, shift, scale):
    return h * (1 + scale[:, None, :]) + shift[:, None, :]


def _pool_mixer(h, w_in, w_grp, scale, w_out):
    z = jnp.einsum('bsd,de->bse', h, w_in)
    S = z.shape[1]
    zf = z.astype(jnp.float32)
    cs = jnp.pad(jnp.cumsum(zf, axis=1), ((0, 0), (1, 0), (0, 0)))
    pos = jnp.arange(S)
    outs = []
    for g, w in enumerate(POOL_WINDOWS):
        sl = slice(g * POOL_GROUP, (g + 1) * POOL_GROUP)
        lo = jnp.maximum(pos - w // 2, 0)
        hi = jnp.minimum(pos + w // 2 - 1, S - 1) + 1
        csg = cs[:, :, sl]
        cnt = (hi - lo).astype(jnp.float32)[None, :, None]
        mean = (csg[:, hi] - csg[:, lo]) / cnt
        diff = (mean - zf[:, :, sl]).astype(h.dtype)
        outs.append(jnp.einsum('bsc,ce->bse', diff, w_grp[g]))
    y = jnp.concatenate(outs, axis=-1) * scale
    return jnp.einsum('bsd,de->bse', y, w_out)


def _sgu_mixer(h, w_in, v_gain, w_s, b_s, w_out):
    z = jax.nn.gelu(jnp.einsum('bsd,df->bsf', h, w_in), approximate=False)
    u, v = jnp.split(z, 2, axis=-1)
    v = _rmsnorm(v, v_gain)
    B, S, _ = v.shape
    n_chunks = S // SGU_CHUNK
    vc = v.reshape(B, n_chunks, SGU_CHUNK, N_SGU_HEADS, SGU_HEAD_DIM)
    mixed = jnp.einsum('hpq,bnqhd->bnphd', w_s, vc) + b_s.T[None, None, :, :, None]
    gated = u * mixed.reshape(B, S, SGU_HALF)
    return jnp.einsum('bsf,fd->bsd', gated, w_out)


def _channel_mlp(h, w1, w2):
    a = jnp.maximum(jnp.einsum('bsd,df->bsf', h, w1), 0)
    return jnp.einsum('bsf,fd->bsd', a * a, w2)


def _trunk(x, c, norm1_g, norm2_g, mod_w, mod_b,
           pool_w_in, pool_w_grp, pool_scale, pool_w_out,
           sgu_w_in, sgu_v_gain, sgu_w_s, sgu_b_s, sgu_w_out,
           mlp_w1, mlp_w2, final_g):
    c_act = jax.nn.silu(c)
    for i in range(DEPTH):
        mod = jnp.einsum('bd,de->be', c_act, mod_w[i]) + mod_b[i]
        sh1, sc1, g1, sh2, sc2, g2 = jnp.split(mod, N_MOD, axis=-1)
        h = _modulate(_rmsnorm(x, norm1_g[i]), sh1, sc1)
        j = i // N_MIXERS
        if i % N_MIXERS == 0:
            y = _pool_mixer(h, pool_w_in[j], pool_w_grp[j], pool_scale[j], pool_w_out[j])
        else:
            y = _sgu_mixer(h, sgu_w_in[j], sgu_v_gain[j], sgu_w_s[j], sgu_b_s[j], sgu_w_out[j])
        x = x + g1[:, None, :] * y
        h = _modulate(_rmsnorm(x, norm2_g[i]), sh2, sc2)
        x = x + g2[:, None, :] * _channel_mlp(h, mlp_w1[i], mlp_w2[i])
    return _rmsnorm(x, final_g)


def setup_inputs(seed: int = 0) -> dict:
    key = jax.random.key(seed)
    ks = jax.random.split(key, 24)
    f32 = jnp.float32
    D = D_MODEL

    def nrm(k, shape, scale):
        return jax.random.normal(k, shape, f32) * scale

    return {
        "x_prompt": nrm(ks[0], (BATCH, SEQ, D), 1.0),
        "x_sample": nrm(ks[1], (DEC_BATCH, DEC_SEQ, D), 1.0),
        "c_prompt": nrm(ks[2], (BATCH, D), 1.0),
        "c_sample": nrm(ks[3], (DEC_BATCH, D), 1.0),
        "norm1_g": 1.0 + nrm(ks[4], (DEPTH, D), 0.05),
        "norm2_g": 1.0 + nrm(ks[5], (DEPTH, D), 0.05),
        "mod_w": nrm(ks[6], (DEPTH, D, N_MOD * D), 0.5 * D ** -0.5),
        "mod_b": nrm(ks[7], (DEPTH, N_MOD * D), 0.01),
        "pool_w_in": nrm(ks[8], (N_POOL_LAYERS, D, D), D ** -0.5),
        "pool_w_grp": nrm(ks[9], (N_POOL_LAYERS, N_POOL_GROUPS, POOL_GROUP, POOL_GROUP), POOL_GROUP ** -0.5),
        "pool_scale": 1.0 + nrm(ks[10], (N_POOL_LAYERS, D), 0.1),
        "pool_w_out": nrm(ks[11], (N_POOL_LAYERS, D, D), D ** -0.5),
        "sgu_w_in": nrm(ks[12], (N_SGU_LAYERS, D, SGU_FF), D ** -0.5),
        "sgu_v_gain": 1.0 + nrm(ks[13], (N_SGU_LAYERS, SGU_HALF), 0.05),
        "sgu_w_s": nrm(ks[14], (N_SGU_LAYERS, N_SGU_HEADS, SGU_CHUNK, SGU_CHUNK), SGU_CHUNK ** -0.5),
        "sgu_b_s": 1.0 + nrm(ks[15], (N_SGU_LAYERS, N_SGU_HEADS, SGU_CHUNK), 0.1),
        "sgu_w_out": nrm(ks[16], (N_SGU_LAYERS, SGU_HALF, D), SGU_HALF ** -0.5),
        "mlp_w1": nrm(ks[17], (DEPTH, D, D_FF), D ** -0.5),
        "mlp_w2": nrm(ks[18], (DEPTH, D_FF, D), D_FF ** -0.5),
        "final_g": 1.0 + nrm(ks[19], (D,), 0.05),
    }


def reference(x_prompt, x_sample, c_prompt, c_sample, norm1_g, norm2_g, mod_w, mod_b,
              pool_w_in, pool_w_grp, pool_scale, pool_w_out,
              sgu_w_in, sgu_v_gain, sgu_w_s, sgu_b_s, sgu_w_out,
              mlp_w1, mlp_w2, final_g):
    y_prompt = _trunk(x_prompt, c_prompt, norm1_g, norm2_g, mod_w, mod_b,
                      pool_w_in, pool_w_grp, pool_scale, pool_w_out,
                      sgu_w_in, sgu_v_gain, sgu_w_s, sgu_b_s, sgu_w_out,
                      mlp_w1, mlp_w2, final_g)
    y_sample = _trunk(x_sample, c_sample, norm1_g, norm2_g, mod_w, mod_b,
                      pool_w_in, pool_w_grp, pool_scale, pool_w_out,
                      sgu_w_in, sgu_v_gain, sgu_w_s, sgu_b_s, sgu_w_out,
                      mlp_w1, mlp_w2, final_g)
    return (y_prompt, y_sample)
```

```python
import functools
import math

import jax
import jax.numpy as jnp
from jax import lax
from jax.experimental import pallas as pl
from jax.experimental.pallas import tpu as pltpu

EPS = 1e-6
POOL_WINDOWS = (2, 4, 8, 16)
SGU_CHUNK = 128
N_SGU_HEADS = 8
N_MOD = 6
MOD_ROWS = 8
HALO = 16
VMEM_LIMIT_BYTES = 58 * 1024 * 1024

F32 = jnp.float32
BF16 = jnp.bfloat16


def _tile(n, target):
    t = min(n, target)
    while n % t:
        t -= 1
    return t


def _rms(x, g):
    ms = jnp.mean(x * x, axis=-1, keepdims=True)
    return (x * lax.rsqrt(ms + EPS)) * g


def _norm_mod(x, g, shift, scale):
    return _rms(x, g) * (1.0 + scale) + shift


def _mod_row(mod_ref, k, row):
    return mod_ref[0, k, pl.ds(row, 1), :]


def _gelu(z):
    return 0.5 * z * (1.0 + lax.erf(z * math.sqrt(0.5)))


def _mod_kernel(c_ref, w_ref, b_ref, o_ref):
    c = c_ref[...]
    c_act = (c * jax.nn.sigmoid(c)).astype(BF16)
    y = jnp.dot(c_act, w_ref[0].astype(BF16), preferred_element_type=F32)
    o_ref[0, 0] = y + b_ref[0]


def _modulation(c_rows, mod_w, mod_b):
    depth, d, _ = mod_w.shape
    tn = _tile(d, 1024)
    nj = d // tn
    return pl.pallas_call(
        _mod_kernel,
        grid=(depth, N_MOD, nj),
        in_specs=[
            pl.BlockSpec((MOD_ROWS, d), lambda l, k, j: (0, 0)),
            pl.BlockSpec((1, d, tn), lambda l, k, j: (l, 0, k * nj + j)),
            pl.BlockSpec((1, 1, tn), lambda l, k, j: (l, 0, k * nj + j)),
        ],
        out_specs=pl.BlockSpec((1, 1, MOD_ROWS, tn), lambda l, k, j: (l, k, 0, j)),
        out_shape=jax.ShapeDtypeStruct((depth, N_MOD, MOD_ROWS, d), F32),
        compiler_params=pltpu.CompilerParams(
            dimension_semantics=("arbitrary", "arbitrary", "arbitrary"),
            vmem_limit_bytes=VMEM_LIMIT_BYTES),
        name="modulation",
    )(c_rows, mod_w, mod_b.reshape(depth, 1, N_MOD * d))


def _mlp_kernel(row_off, final, x_ref, mod_ref, g_ref, w1_ref, w2_ref, fg_ref, o_ref, h_ref):
    f = pl.program_id(2)
    row = pl.program_id(0) + row_off

    @pl.when(f == 0)
    def _():
        h = _norm_mod(x_ref[0], g_ref[0], _mod_row(mod_ref, 3, row), _mod_row(mod_ref, 4, row))
        h_ref[...] = h.astype(BF16)
        o_ref[0] = jnp.zeros(o_ref.shape[1:], F32)

    a = jnp.dot(h_ref[...], w1_ref[0], preferred_element_type=F32)
    a = jnp.maximum(a, 0.0)
    a = (a * a).astype(BF16)
    o_ref[0] += jnp.dot(a, w2_ref[0], preferred_element_type=F32)

    @pl.when(f == pl.num_programs(2) - 1)
    def _():
        xn = x_ref[0] + _mod_row(mod_ref, 5, row) * o_ref[0]
        if final:
            xn = _rms(xn, fg_ref[...])
        o_ref[0] = xn


def _mlp_layer(x, mod, layer, row_off, norm_g, w1, w2, final_g, final):
    b, s, d = x.shape
    d_ff = w1.shape[-1]
    tm = _tile(s, 512)
    tf = _tile(d_ff, 1024)
    return pl.pallas_call(
        functools.partial(_mlp_kernel, row_off, final),
        grid=(b, s // tm, d_ff // tf),
        in_specs=[
            pl.BlockSpec((1, tm, d), lambda bi, i, f: (bi, i, 0)),
            pl.BlockSpec((1, N_MOD, MOD_ROWS, d), lambda bi, i, f: (layer, 0, 0, 0)),
            pl.BlockSpec((1, 1, d), lambda bi, i, f: (layer, 0, 0)),
            pl.BlockSpec((1, d, tf), lambda bi, i, f: (layer, 0, f)),
            pl.BlockSpec((1, tf, d), lambda bi, i, f: (layer, f, 0)),
            pl.BlockSpec((1, d), lambda bi, i, f: (0, 0)),
        ],
        out_specs=pl.BlockSpec((1, tm, d), lambda bi, i, f: (bi, i, 0)),
        out_shape=jax.ShapeDtypeStruct((b, s, d), F32),
        scratch_shapes=[pltpu.VMEM((tm, d), BF16)],
        compiler_params=pltpu.CompilerParams(
            dimension_semantics=("arbitrary", "arbitrary", "arbitrary"),
            vmem_limit_bytes=VMEM_LIMIT_BYTES),
        name="mlp_final" if final else "mlp",
    )(x, mod, norm_g, w1, w2, final_g)


def _pool_kernel(row_off, seq, x_ref, xp_ref, xn_ref, mod_ref, g_ref, win_ref, wgrp_ref,
                 scale_ref, wout_ref, o_ref, h_ref, z_ref):
    i = pl.program_id(1)
    row = pl.program_id(0) + row_off
    tm = x_ref.shape[1]
    d = x_ref.shape[2]
    pg = d // len(POOL_WINDOWS)
    g = g_ref[0]
    shift = _mod_row(mod_ref, 0, row)
    scale = _mod_row(mod_ref, 1, row)

    h_ref[0:HALO] = _norm_mod(xp_ref[0], g, shift, scale).astype(BF16)
    h_ref[HALO:HALO + tm] = _norm_mod(x_ref[0], g, shift, scale).astype(BF16)
    h_ref[HALO + tm:] = _norm_mod(xn_ref[0], g, shift, scale).astype(BF16)

    z = jnp.dot(h_ref[...], win_ref[0], preferred_element_type=F32)
    pos_h = i * tm - HALO + lax.broadcasted_iota(jnp.int32, (tm + 2 * HALO, 1), 0)
    z_ref[...] = jnp.where((pos_h >= 0) & (pos_h < seq), z, 0.0)

    pos = i * tm + lax.broadcasted_iota(jnp.int32, (tm, 1), 0)
    parts = []
    for gi, w in enumerate(POOL_WINDOWS):
        cols = slice(gi * pg, (gi + 1) * pg)
        acc = z_ref[pl.ds(HALO - w // 2, tm), cols]
        for k in range(-w // 2 + 1, w // 2):
            acc = acc + z_ref[pl.ds(HALO + k, tm), cols]
        lo = jnp.maximum(pos - w // 2, 0)
        hi = jnp.minimum(pos + w // 2 - 1, seq - 1) + 1
        cnt = (hi - lo).astype(F32)
        diff = acc / cnt - z_ref[pl.ds(HALO, tm), cols]
        yg = jnp.dot(diff.astype(BF16), wgrp_ref[0, gi], preferred_element_type=F32)
        parts.append((yg * scale_ref[0, :, cols]).astype(BF16))
    y = jnp.concatenate(parts, axis=-1)
    out = jnp.dot(y, wout_ref[0], preferred_element_type=F32)
    o_ref[0] = x_ref[0] + _mod_row(mod_ref, 2, row) * out


def _pool_layer(x, mod, layer, row_off, norm_g, j, w_in, w_grp, scale, w_out):
    b, s, d = x.shape
    tm = _tile(s, 512)
    assert tm % HALO == 0 and max(POOL_WINDOWS) // 2 <= HALO
    nh = tm // HALO
    n_halo_blocks = s // HALO
    ng = len(POOL_WINDOWS)
    pg = d // ng
    const = dict(pipeline_mode=pl.Buffered(1))
    return pl.pallas_call(
        functools.partial(_pool_kernel, row_off, s),
        grid=(b, s // tm),
        in_specs=[
            pl.BlockSpec((1, tm, d), lambda bi, i: (bi, i, 0)),
            pl.BlockSpec((1, HALO, d), lambda bi, i: (bi, jnp.maximum(i * nh - 1, 0), 0)),
            pl.BlockSpec((1, HALO, d),
                         lambda bi, i: (bi, jnp.minimum((i + 1) * nh, n_halo_blocks - 1), 0)),
            pl.BlockSpec((1, N_MOD, MOD_ROWS, d), lambda bi, i: (layer, 0, 0, 0)),
            pl.BlockSpec((1, 1, d), lambda bi, i: (layer, 0, 0)),
            pl.BlockSpec((1, d, d), lambda bi, i: (j, 0, 0), **const),
            pl.BlockSpec((1, ng, pg, pg), lambda bi, i: (j, 0, 0, 0), **const),
            pl.BlockSpec((1, 1, d), lambda bi, i: (j, 0, 0)),
            pl.BlockSpec((1, d, d), lambda bi, i: (j, 0, 0), **const),
        ],
        out_specs=pl.BlockSpec((1, tm, d), lambda bi, i: (bi, i, 0)),
        out_shape=jax.ShapeDtypeStruct((b, s, d), F32),
        scratch_shapes=[pltpu.VMEM((tm + 2 * HALO, d), BF16),
                        pltpu.VMEM((tm + 2 * HALO, d), F32)],
        compiler_params=pltpu.CompilerParams(
            dimension_semantics=("arbitrary", "arbitrary"),
            vmem_limit_bytes=VMEM_LIMIT_BYTES),
        name="pool_mixer",
    )(x, x, x, mod, norm_g, w_in, w_grp, scale, w_out)


def _sgu_kernel(row_off, half, x_ref, mod_ref, g_ref, win_ref, vgain_ref, ws_ref, bs_ref,
                wout_ref, o_ref, h_ref, v_ref, ss_ref):
    s = pl.program_id(2)
    row = pl.program_id(0) + row_off
    tm = x_ref.shape[1]
    nh = N_SGU_HEADS

    @pl.when(s == 0)
    def _():
        h = _norm_mod(x_ref[0], g_ref[0], _mod_row(mod_ref, 0, row), _mod_row(mod_ref, 1, row))
        h_ref[...] = h.astype(BF16)
        ss_ref[...] = jnp.zeros(ss_ref.shape, F32)
        o_ref[0] = jnp.zeros(o_ref.shape[1:], F32)

    z = _gelu(jnp.dot(h_ref[...], win_ref[0], preferred_element_type=F32))

    @pl.when(s < nh)
    def _():
        v_ref[s] = z
        ss_ref[...] += jnp.sum(z * z, axis=-1, keepdims=True)

    @pl.when(s >= nh)
    def _():
        inv = lax.rsqrt(ss_ref[...] / half + EPS)
        vn = ((v_ref[s - nh] * inv) * vgain_ref[0]).astype(BF16)
        ws = ws_ref[0, 0]
        bias = bs_ref[0, 0]
        mixed = [jnp.dot(ws, vn[c * SGU_CHUNK:(c + 1) * SGU_CHUNK], preferred_element_type=F32) + bias
                 for c in range(tm // SGU_CHUNK)]
        gated = (z * jnp.concatenate(mixed, axis=0)).astype(BF16)
        o_ref[0] += jnp.dot(gated, wout_ref[0], preferred_element_type=F32)

    @pl.when(s == 2 * nh - 1)
    def _():
        o_ref[0] = x_ref[0] + _mod_row(mod_ref, 2, row) * o_ref[0]


def _sgu_layer(x, mod, layer, row_off, norm_g, j, w_in, v_gain, w_s, b_s, w_out):
    b, s, d = x.shape
    half = w_out.shape[1]
    nh = N_SGU_HEADS
    hd = half // nh
    tm = _tile(s, 512)
    assert tm % SGU_CHUNK == 0
    return pl.pallas_call(
        functools.partial(_sgu_kernel, row_off, half),
        grid=(b, s // tm, 2 * nh),
        in_specs=[
            pl.BlockSpec((1, tm, d), lambda bi, i, t: (bi, i, 0)),
            pl.BlockSpec((1, N_MOD, MOD_ROWS, d), lambda bi, i, t: (layer, 0, 0, 0)),
            pl.BlockSpec((1, 1, d), lambda bi, i, t: (layer, 0, 0)),
            pl.BlockSpec((1, d, hd), lambda bi, i, t: (j, 0, (t + nh) % (2 * nh))),
            pl.BlockSpec((1, 1, hd), lambda bi, i, t: (j, 0, jnp.maximum(t - nh, 0))),
            pl.BlockSpec((1, 1, SGU_CHUNK, SGU_CHUNK), lambda bi, i, t: (j, jnp.maximum(t - nh, 0), 0, 0)),
            pl.BlockSpec((1, 1, SGU_CHUNK, 1), lambda bi, i, t: (j, jnp.maximum(t - nh, 0), 0, 0)),
            pl.BlockSpec((1, hd, d), lambda bi, i, t: (j, jnp.maximum(t - nh, 0), 0)),
        ],
        out_specs=pl.BlockSpec((1, tm, d), lambda bi, i, t: (bi, i, 0)),
        out_shape=jax.ShapeDtypeStruct((b, s, d), F32),
        scratch_shapes=[pltpu.VMEM((tm, d), BF16),
                        pltpu.VMEM((nh, tm, hd), F32),
                        pltpu.VMEM((tm, 1), F32)],
        compiler_params=pltpu.CompilerParams(
            dimension_semantics=("arbitrary", "arbitrary", "arbitrary"),
            vmem_limit_bytes=VMEM_LIMIT_BYTES),
        name="sgu_mixer",
    )(x, mod, norm_g, w_in, v_gain, w_s, b_s, w_out)


def kernel(x_prompt, x_sample, c_prompt, c_sample, norm1_g, norm2_g, mod_w, mod_b, pool_w_in, pool_w_grp, pool_scale, pool_w_out, sgu_w_in, sgu_v_gain, sgu_w_s, sgu_b_s, sgu_w_out, mlp_w1, mlp_w2, final_g):
    depth, d = norm1_g.shape
    n_prompt = c_prompt.shape[0]
    n_rows = n_prompt + c_sample.shape[0]
    assert n_rows <= MOD_ROWS
    c_rows = jnp.concatenate([c_prompt, c_sample, jnp.zeros((MOD_ROWS - n_rows, d), F32)], axis=0)
    mod = _modulation(c_rows, mod_w, mod_b)

    norm1 = norm1_g.reshape(depth, 1, d)
    norm2 = norm2_g.reshape(depth, 1, d)
    fg = final_g.reshape(1, d)
    pool_w_in_b = pool_w_in.astype(BF16)
    pool_w_grp_b = pool_w_grp.astype(BF16)
    pool_w_out_b = pool_w_out.astype(BF16)
    pool_scale_r = pool_scale.reshape(pool_scale.shape[0], 1, d)
    sgu_w_in_b = sgu_w_in.astype(BF16)
    sgu_w_s_b = sgu_w_s.astype(BF16)
    sgu_w_out_b = sgu_w_out.astype(BF16)
    sgu_v_gain_r = sgu_v_gain.reshape(sgu_v_gain.shape[0], 1, -1)
    sgu_b_s_r = sgu_b_s.reshape(sgu_b_s.shape + (1,))
    mlp_w1_b = mlp_w1.astype(BF16)
    mlp_w2_b = mlp_w2.astype(BF16)

    def trunk(x, row_off):
        for layer in range(depth):
            j = layer // 2
            if layer % 2 == 0:
                x = _pool_layer(x, mod, layer, row_off, norm1, j, pool_w_in_b, pool_w_grp_b,
                                pool_scale_r, pool_w_out_b)
            else:
                x = _sgu_layer(x, mod, layer, row_off, norm1, j, sgu_w_in_b, sgu_v_gain_r,
                               sgu_w_s_b, sgu_b_s_r, sgu_w_out_b)
            x = _mlp_layer(x, mod, layer, row_off, norm2, mlp_w1_b, mlp_w2_b, fg,
                           final=(layer == depth - 1))
        return x

    return (trunk(x_prompt, 0), trunk(x_sample, n_prompt))
```

```python
import functools
import math

import jax
import jax.numpy as jnp
from jax import lax
from jax.experimental import pallas as pl
from jax.experimental.pallas import tpu as pltpu

EPS = 1e-6
POOL_WINDOWS = (2, 4, 8, 16)
SGU_CHUNK = 128
N_SGU_HEADS = 8
MIX_CHUNKS = 2
N_MOD = 6
MOD_ROWS = 8
ROW_ALIGN = 16
HALO = ROW_ALIGN
VMEM_LIMIT_BYTES = 58 * 1024 * 1024

F32 = jnp.float32
BF16 = jnp.bfloat16


def _tile(n, target):
    t = min(n, target)
    while n % t:
        t -= 1
    return t


def _rms(x, g):
    ms = jnp.mean(x * x, axis=-1, keepdims=True)
    return (x * lax.rsqrt(ms + EPS)) * g


def _norm_mod(x, g, shift, scale):
    return _rms(x, g) * (1.0 + scale) + shift


def _mod_row(mod_ref, k, row):
    return mod_ref[0, k, pl.ds(row, 1), :]


def _gelu(z):
    return 0.5 * z * (1.0 + lax.erf(z * math.sqrt(0.5)))


def _mod_kernel(c_ref, w_ref, b_ref, o_ref):
    c = c_ref[...]
    c_act = (c * jax.nn.sigmoid(c)).astype(BF16)
    y = jnp.dot(c_act, w_ref[0].astype(BF16), preferred_element_type=F32)
    o_ref[0, 0] = y + b_ref[0]


def _modulation(c_rows, mod_w, mod_b):
    depth, d, _ = mod_w.shape
    tn = _tile(d, 1024)
    nj = d // tn
    return pl.pallas_call(
        _mod_kernel,
        grid=(depth, N_MOD, nj),
        in_specs=[
            pl.BlockSpec((MOD_ROWS, d), lambda l, k, j: (0, 0)),
            pl.BlockSpec((1, d, tn), lambda l, k, j: (l, 0, k * nj + j)),
            pl.BlockSpec((1, 1, tn), lambda l, k, j: (l, 0, k * nj + j)),
        ],
        out_specs=pl.BlockSpec((1, 1, MOD_ROWS, tn), lambda l, k, j: (l, k, 0, j)),
        out_shape=jax.ShapeDtypeStruct((depth, N_MOD, MOD_ROWS, d), F32),
        compiler_params=pltpu.CompilerParams(
            dimension_semantics=("arbitrary", "arbitrary", "arbitrary"),
            vmem_limit_bytes=VMEM_LIMIT_BYTES),
        name="modulation",
    )(c_rows, mod_w, mod_b.reshape(depth, 1, N_MOD * d))


def _prologue_rows(n_steps, tm):
    per_step = -(-tm // (n_steps - 1))
    return min(tm, -(-per_step // ROW_ALIGN) * ROW_ALIGN)


def _next_tile_prologue(chunk, rows, x_ref, h_next_ref, g, shift, scale):
    tm = x_ref.shape[0]
    start = pl.multiple_of(jnp.minimum(chunk * rows, tm - rows), ROW_ALIGN)
    h = _norm_mod(x_ref[pl.ds(start, rows), :], g, shift, scale)
    h_next_ref[pl.ds(start, rows), :] = h.astype(BF16)
    bits = pltpu.bitcast(h, jnp.int32)
    half_word = jnp.full(bits.shape, 16, jnp.int32)
    zero = lax.shift_right_logical(lax.shift_right_logical(bits, half_word), half_word)
    return jnp.sum(zero, axis=0, keepdims=True).astype(F32)


def _start_tile(n, x_ref, o_ref, h_ref, hn_ref, g, shift, scale):
    o_ref[...] = x_ref[...]

    @pl.when(n == 0)
    def _():
        h_ref[...] = _norm_mod(x_ref[...], g, shift, scale).astype(BF16)

    @pl.when(n > 0)
    def _():
        h_ref[...] = hn_ref[...]


def _mlp_kernel(row_off, tiles_per_row, n_rows, n_steps, final, x_ref, mod_ref, g_ref, w1_ref, w2_ref,
                fg_ref, o_ref, h_ref, hn_ref):
    n = pl.program_id(0)
    f = pl.program_id(1)
    tm = x_ref.shape[0]
    row = n // tiles_per_row + row_off
    row_next = jnp.minimum((n + 1) // tiles_per_row, n_rows - 1) + row_off
    g = g_ref[0]

    @pl.when(f == 0)
    def _():
        _start_tile(n, x_ref, o_ref, h_ref, hn_ref, g, _mod_row(mod_ref, 3, row), _mod_row(mod_ref, 4, row))

    zero = _next_tile_prologue(jnp.maximum(f - 1, 0), _prologue_rows(n_steps, tm), x_ref, hn_ref, g,
                               _mod_row(mod_ref, 3, row_next), _mod_row(mod_ref, 4, row_next))
    a = jnp.dot(h_ref[...], w1_ref[0], preferred_element_type=F32)
    a = jnp.maximum(a, 0.0)
    a = a * a
    head = a.shape[1] // 4
    a = jnp.concatenate([a[:, :head] + zero[:, :head], a[:, head:]], axis=1).astype(BF16)
    o_ref[...] += _mod_row(mod_ref, 5, row) * jnp.dot(a, w2_ref[0], preferred_element_type=F32)

    if final:
        @pl.when(f == n_steps - 1)
        def _():
            o_ref[...] = _rms(o_ref[...], fg_ref[...])


def _mlp_layer(x, mod, layer, row_off, norm_g, w1, w2, final_g, final):
    b, s, d = x.shape
    d_ff = w1.shape[-1]
    tm = _tile(s, 512)
    tf = _tile(d_ff, min(1024, d_ff // 2))
    nt = s // tm
    n_tiles = b * nt
    nf = d_ff // tf
    out = pl.pallas_call(
        functools.partial(_mlp_kernel, row_off, nt, b, nf, final),
        grid=(n_tiles, nf),
        in_specs=[
            pl.BlockSpec((tm, d), lambda n, f: (jnp.minimum(n + jnp.minimum(f, 1), n_tiles - 1), 0)),
            pl.BlockSpec((1, N_MOD, MOD_ROWS, d), lambda n, f: (layer, 0, 0, 0)),
            pl.BlockSpec((1, 1, d), lambda n, f: (layer, 0, 0)),
            pl.BlockSpec((1, d, tf), lambda n, f: (layer, 0, f)),
            pl.BlockSpec((1, tf, d), lambda n, f: (layer, f, 0)),
            pl.BlockSpec((1, d), lambda n, f: (0, 0)),
        ],
        out_specs=pl.BlockSpec((tm, d), lambda n, f: (n, 0)),
        out_shape=jax.ShapeDtypeStruct((b * s, d), F32),
        scratch_shapes=[pltpu.VMEM((tm, d), BF16), pltpu.VMEM((tm, d), BF16)],
        compiler_params=pltpu.CompilerParams(
            dimension_semantics=("arbitrary", "arbitrary"),
            vmem_limit_bytes=VMEM_LIMIT_BYTES),
        name="mlp_final" if final else "mlp",
    )(x.reshape(b * s, d), mod, norm_g, w1, w2, final_g)
    return out.reshape(b, s, d)


def _pool_kernel(row_off, seq, x_ref, xp_ref, xn_ref, mod_ref, g_ref, win_ref, wgrp_ref,
                 scale_ref, wout_ref, o_ref, h_ref, z_ref):
    i = pl.program_id(1)
    row = pl.program_id(0) + row_off
    tm = x_ref.shape[1]
    d = x_ref.shape[2]
    pg = d // len(POOL_WINDOWS)
    g = g_ref[0]
    shift = _mod_row(mod_ref, 0, row)
    scale = _mod_row(mod_ref, 1, row)

    h_ref[0:HALO] = _norm_mod(xp_ref[0], g, shift, scale).astype(BF16)
    h_ref[HALO:HALO + tm] = _norm_mod(x_ref[0], g, shift, scale).astype(BF16)
    h_ref[HALO + tm:] = _norm_mod(xn_ref[0], g, shift, scale).astype(BF16)

    z = jnp.dot(h_ref[...], win_ref[0], preferred_element_type=F32)
    pos_h = i * tm - HALO + lax.broadcasted_iota(jnp.int32, (tm + 2 * HALO, 1), 0)
    z_ref[...] = jnp.where((pos_h >= 0) & (pos_h < seq), z, 0.0)

    pos = i * tm + lax.broadcasted_iota(jnp.int32, (tm, 1), 0)
    parts = []
    for gi, w in enumerate(POOL_WINDOWS):
        cols = slice(gi * pg, (gi + 1) * pg)
        acc = z_ref[pl.ds(HALO - w // 2, tm), cols]
        for k in range(-w // 2 + 1, w // 2):
            acc = acc + z_ref[pl.ds(HALO + k, tm), cols]
        lo = jnp.maximum(pos - w // 2, 0)
        hi = jnp.minimum(pos + w // 2 - 1, seq - 1) + 1
        inv_cnt = 1.0 / (hi - lo).astype(F32)
        diff = acc * inv_cnt - z_ref[pl.ds(HALO, tm), cols]
        yg = jnp.dot(diff.astype(BF16), wgrp_ref[0, gi], preferred_element_type=F32)
        parts.append((yg * scale_ref[0, :, cols]).astype(BF16))
    y = jnp.concatenate(parts, axis=-1)
    out = jnp.dot(y, wout_ref[0], preferred_element_type=F32)
    o_ref[0] = x_ref[0] + _mod_row(mod_ref, 2, row) * out


def _pool_layer(x, mod, layer, row_off, norm_g, j, w_in, w_grp, scale, w_out):
    b, s, d = x.shape
    tm = _tile(s, 512)
    assert tm % HALO == 0 and max(POOL_WINDOWS) // 2 <= HALO
    nh = tm // HALO
    n_halo_blocks = s // HALO
    ng = len(POOL_WINDOWS)
    pg = d // ng
    const = dict(pipeline_mode=pl.Buffered(1))
    return pl.pallas_call(
        functools.partial(_pool_kernel, row_off, s),
        grid=(b, s // tm),
        in_specs=[
            pl.BlockSpec((1, tm, d), lambda bi, i: (bi, i, 0)),
            pl.BlockSpec((1, HALO, d), lambda bi, i: (bi, jnp.maximum(i * nh - 1, 0), 0)),
            pl.BlockSpec((1, HALO, d),
                         lambda bi, i: (bi, jnp.minimum((i + 1) * nh, n_halo_blocks - 1), 0)),
            pl.BlockSpec((1, N_MOD, MOD_ROWS, d), lambda bi, i: (layer, 0, 0, 0)),
            pl.BlockSpec((1, 1, d), lambda bi, i: (layer, 0, 0)),
            pl.BlockSpec((1, d, d), lambda bi, i: (j, 0, 0), **const),
            pl.BlockSpec((1, ng, pg, pg), lambda bi, i: (j, 0, 0, 0), **const),
            pl.BlockSpec((1, 1, d), lambda bi, i: (j, 0, 0)),
            pl.BlockSpec((1, d, d), lambda bi, i: (j, 0, 0), **const),
        ],
        out_specs=pl.BlockSpec((1, tm, d), lambda bi, i: (bi, i, 0)),
        out_shape=jax.ShapeDtypeStruct((b, s, d), F32),
        scratch_shapes=[pltpu.VMEM((tm + 2 * HALO, d), BF16),
                        pltpu.VMEM((tm + 2 * HALO, d), F32)],
        compiler_params=pltpu.CompilerParams(
            dimension_semantics=("arbitrary", "arbitrary"),
            vmem_limit_bytes=VMEM_LIMIT_BYTES),
        name="pool_mixer",
    )(x, x, x, mod, norm_g, w_in, w_grp, scale, w_out)


def _sgu_kernel(row_off, tiles_per_row, n_rows, half, x_ref, mod_ref, g_ref, win_ref, vgain_ref,
                ws_ref, bs_ref, wout_ref, o_ref, h_ref, hn_ref, v_ref, ss_ref, raw_ref):
    n = pl.program_id(0)
    s = pl.program_id(1)
    tm = x_ref.shape[0]
    nh = N_SGU_HEADS
    row = n // tiles_per_row + row_off
    row_next = jnp.minimum((n + 1) // tiles_per_row, n_rows - 1) + row_off
    g = g_ref[0]

    def project(zero=None):
        raw = jnp.dot(h_ref[...], win_ref[0], preferred_element_type=F32)
        if zero is not None:
            raw = raw + zero
        raw_ref[s % 2] = raw

    def finish_v():
        z = _gelu(raw_ref[(s - 1) % 2])
        v_ref[s - 1] = z
        sq = jnp.sum(z * z, axis=-1, keepdims=True)
        ss_ref[...] += sq
        bits = pltpu.bitcast(sq, jnp.int32)
        half_word = jnp.full(bits.shape, 16, jnp.int32)
        return lax.shift_right_logical(lax.shift_right_logical(bits, half_word), half_word).astype(F32)

    def finish_u():
        k = s - 1 - nh
        zero = _next_tile_prologue(k, tm // nh, x_ref, hn_ref, g,
                                   _mod_row(mod_ref, 0, row_next), _mod_row(mod_ref, 1, row_next))
        z = _gelu(raw_ref[(s - 1) % 2])
        head = z.shape[1] // 3
        z = jnp.concatenate([z[:, :head] + zero[:, :head], z[:, head:]], axis=1)
        inv = lax.rsqrt(ss_ref[...] / half + EPS)
        vn = ((v_ref[k] * inv) * vgain_ref[0]).astype(BF16)
        ws = ws_ref[0, 0]
        bias = bs_ref[0, 0]
        mr = ws.shape[0]
        mixed = jnp.concatenate(
            [jnp.dot(ws, vn[r * mr:(r + 1) * mr], preferred_element_type=F32) + bias
             for r in range(tm // mr)], axis=0)
        gated = (z * mixed).astype(BF16)
        o_ref[...] += _mod_row(mod_ref, 2, row) * jnp.dot(gated, wout_ref[0], preferred_element_type=F32)

    @pl.when(s == 0)
    def _():
        _start_tile(n, x_ref, o_ref, h_ref, hn_ref, g, _mod_row(mod_ref, 0, row), _mod_row(mod_ref, 1, row))
        ss_ref[...] = jnp.zeros(ss_ref.shape, F32)
        project()

    @pl.when((s >= 1) & (s <= nh))
    def _():
        project(finish_v())

    @pl.when((s > nh) & (s < 2 * nh))
    def _():
        finish_u()
        project()

    @pl.when(s == 2 * nh)
    def _():
        finish_u()


def _sgu_layer(x, mod, layer, row_off, norm_g, j, w_in, v_gain, w_s, b_s, w_out):
    b, s, d = x.shape
    half = w_out.shape[1]
    nh = N_SGU_HEADS
    hd = half // nh
    tm = _tile(s, 512)
    mix_rows = w_s.shape[-1]
    assert tm % mix_rows == 0
    nt = s // tm
    n_tiles = b * nt

    def u_head(t):
        return jnp.clip(t - 1 - nh, 0, nh - 1)

    out = pl.pallas_call(
        functools.partial(_sgu_kernel, row_off, nt, b, half),
        grid=(n_tiles, 2 * nh + 1),
        in_specs=[
            pl.BlockSpec((tm, d), lambda n, t: (jnp.minimum(n + t // (nh + 1), n_tiles - 1), 0)),
            pl.BlockSpec((1, N_MOD, MOD_ROWS, d), lambda n, t: (layer, 0, 0, 0)),
            pl.BlockSpec((1, 1, d), lambda n, t: (layer, 0, 0)),
            pl.BlockSpec((1, d, hd), lambda n, t: (j, 0, (jnp.minimum(t, 2 * nh - 1) + nh) % (2 * nh))),
            pl.BlockSpec((1, 1, hd), lambda n, t: (j, 0, u_head(t))),
            pl.BlockSpec((1, 1, mix_rows, mix_rows), lambda n, t: (j, u_head(t), 0, 0)),
            pl.BlockSpec((1, 1, mix_rows, 1), lambda n, t: (j, u_head(t), 0, 0)),
            pl.BlockSpec((1, hd, d), lambda n, t: (j, u_head(t), 0)),
        ],
        out_specs=pl.BlockSpec((tm, d), lambda n, t: (n, 0)),
        out_shape=jax.ShapeDtypeStruct((b * s, d), F32),
        scratch_shapes=[pltpu.VMEM((tm, d), BF16),
                        pltpu.VMEM((tm, d), BF16),
                        pltpu.VMEM((nh, tm, hd), F32),
                        pltpu.VMEM((tm, 1), F32),
                        pltpu.VMEM((2, tm, hd), F32)],
        compiler_params=pltpu.CompilerParams(
            dimension_semantics=("arbitrary", "arbitrary"),
            vmem_limit_bytes=VMEM_LIMIT_BYTES),
        name="sgu_mixer",
    )(x.reshape(b * s, d), mod, norm_g, w_in, v_gain, w_s, b_s, w_out)
    return out.reshape(b, s, d)


def kernel(x_prompt, x_sample, c_prompt, c_sample, norm1_g, norm2_g, mod_w, mod_b, pool_w_in, pool_w_grp, pool_scale, pool_w_out, sgu_w_in, sgu_v_gain, sgu_w_s, sgu_b_s, sgu_w_out, mlp_w1, mlp_w2, final_g):
    depth, d = norm1_g.shape
    n_prompt = c_prompt.shape[0]
    n_rows = n_prompt + c_sample.shape[0]
    assert n_rows <= MOD_ROWS
    c_rows = jnp.concatenate([c_prompt, c_sample, jnp.zeros((MOD_ROWS - n_rows, d), F32)], axis=0)
    mod = _modulation(c_rows, mod_w, mod_b)

    norm1 = norm1_g.reshape(depth, 1, d)
    norm2 = norm2_g.reshape(depth, 1, d)
    fg = final_g.reshape(1, d)
    pool_w_in_b = pool_w_in.astype(BF16)
    pool_w_grp_b = pool_w_grp.astype(BF16)
    pool_w_out_b = pool_w_out.astype(BF16)
    pool_scale_r = pool_scale.reshape(pool_scale.shape[0], 1, d)
    sgu_w_in_b = sgu_w_in.astype(BF16)
    eye = jnp.eye(MIX_CHUNKS, dtype=F32)
    sgu_w_s_b = jnp.einsum('ab,lhpq->lhapbq', eye, sgu_w_s).reshape(
        sgu_w_s.shape[:2] + (MIX_CHUNKS * SGU_CHUNK, MIX_CHUNKS * SGU_CHUNK)).astype(BF16)
    sgu_w_out_b = sgu_w_out.astype(BF16)
    sgu_v_gain_r = sgu_v_gain.reshape(sgu_v_gain.shape[0], 1, -1)
    sgu_b_s_r = jnp.tile(sgu_b_s, (1, 1, MIX_CHUNKS)).reshape(sgu_b_s.shape[:2] + (MIX_CHUNKS * SGU_CHUNK, 1))
    mlp_w1_b = mlp_w1.astype(BF16)
    mlp_w2_b = mlp_w2.astype(BF16)

    def trunk(x, row_off):
        for layer in range(depth):
            j = layer // 2
            if layer % 2 == 0:
                x = _pool_layer(x, mod, layer, row_off, norm1, j, pool_w_in_b, pool_w_grp_b,
                                pool_scale_r, pool_w_out_b)
            else:
                x = _sgu_layer(x, mod, layer, row_off, norm1, j, sgu_w_in_b, sgu_v_gain_r,
                               sgu_w_s_b, sgu_b_s_r, sgu_w_out_b)
            x = _mlp_layer(x, mod, layer, row_off, norm2, mlp_w1_b, mlp_w2_b, fg,
                           final=(layer == depth - 1))
        return x

    return (trunk(x_prompt, 0), trunk(x_sample, n_prompt))
```
